```python
import math
import jax, jax.numpy as jnp
from jax import lax
import numpy as np

D_MODEL = 1024
BATCH = 32
SEQ = 2048
DEPTH = 2
DEC_BATCH = 8
DEC_SEQ = 4096
PAST_LEN = 128

ATT_HEADS = 8
ATT_HEAD_DIM = 64
ATT_V_DIM = 2 * ATT_HEAD_DIM
ATT_WIDTH = ATT_HEADS * ATT_V_DIM
QBLOCK = 128
NUM_BUCKETS = 32
MAX_DISTANCE = 128
SSM_INNER = 1024
SSM_HEAD_DIM = 64
SSM_HEADS = SSM_INNER // SSM_HEAD_DIM
SSM_GROUPS = 2
SSM_STATE = 128
D_CONV = 5
CHUNK = 128
CONV_CH = SSM_INNER + 2 * SSM_GROUPS * SSM_STATE
D_FF = ((8 * D_MODEL // 3 + 255) // 256) * 256
Q_COLS = ATT_HEADS * 2 * ATT_HEAD_DIM
K_COLS = ATT_HEADS * 2 * ATT_HEAD_DIM
V_COLS = ATT_WIDTH
Z_COLS = SSM_INNER
XBC_COLS = CONV_CH
DT_COLS = 2 * SSM_HEADS
GATE_COLS = 2 * D_MODEL
IN_COLS = Q_COLS + K_COLS + V_COLS + Z_COLS + XBC_COLS + DT_COLS + GATE_COLS
SPLIT_POINTS = (Q_COLS, Q_COLS + K_COLS, Q_COLS + K_COLS + V_COLS,
                Q_COLS + K_COLS + V_COLS + Z_COLS,
                Q_COLS + K_COLS + V_COLS + Z_COLS + XBC_COLS,
                Q_COLS + K_COLS + V_COLS + Z_COLS + XBC_COLS + DT_COLS)
EPS = 1e-6

kernel_name = "hybrid_diffattn_ssd_encoder"


def rmsnorm(x, w):
    x32 = x.astype(jnp.float32)
    y = x32 * lax.rsqrt(jnp.mean(x32 * x32, axis=-1, keepdims=True) + EPS)
    return (y * w.astype(jnp.float32)).astype(x.dtype)


def rel_bucket(rel):
    nb = NUM_BUCKETS // 2
    ret = (rel > 0).astype(jnp.int32) * nb
    n = jnp.abs(rel)
    max_exact = nb // 2
    nf = jnp.maximum(n, 1).astype(jnp.float32)
    large = max_exact + (jnp.log(nf / max_exact) / math.log(MAX_DISTANCE / max_exact)
                         * (nb - max_exact)).astype(jnp.int32)
    large = jnp.minimum(large, nb - 1)
    return ret + jnp.where(n < max_exact, n, large)


def diff_attention(q, k, v, lam, rel_bias):
    b, S = q.shape[0], q.shape[1]
    nb = S // QBLOCK
    scale = ATT_HEAD_DIM ** -0.5
    qb = jnp.moveaxis(q.reshape(b, nb, QBLOCK, ATT_HEADS, 2, ATT_HEAD_DIM), 1, 0)
    starts = jnp.arange(nb, dtype=jnp.int32) * QBLOCK
    k_pos = jnp.arange(S, dtype=jnp.int32)
    kf = k.astype(jnp.float32)
    vf = v.astype(jnp.float32)
    table = rel_bias.astype(jnp.float32)

    def block(args):
        qblk, start = args
        q_pos = start + jnp.arange(QBLOCK, dtype=jnp.int32)
        bias = table[rel_bucket(k_pos[None, :] - q_pos[:, None])]
        bias = jnp.transpose(bias, (2, 0, 1))
        s = jnp.einsum('bqhmd,bkhmd->bhmqk', qblk.astype(jnp.float32), kf) * scale
        p = jax.nn.softmax(s + bias[None, :, None], axis=-1)
        a = p[:, :, 0] - lam * p[:, :, 1]
        return jnp.einsum('bhqk,bkhe->bqhe', a, vf)

    out = lax.map(block, (qb, starts))
    return jnp.moveaxis(out, 0, 1).reshape(b, S, ATT_HEADS, ATT_V_DIM)


def ssd_scan(x, dt, A, Bm, Cm):
    b, L, h, p = x.shape
    g, n = Bm.shape[2], Bm.shape[3]
    r = h // g
    c = L // CHUNK
    f32 = jnp.float32
    dt = dt.astype(f32)
    xdt = (x.astype(f32) * dt[..., None]).reshape(b, c, CHUNK, g, r, p)
    a = (dt * A.astype(f32)).reshape(b, c, CHUNK, g, r).transpose(0, 3, 4, 1, 2)
    Bc = Bm.astype(f32).reshape(b, c, CHUNK, g, n)
    Cc = Cm.astype(f32).reshape(b, c, CHUNK, g, n)
    a_cum = jnp.cumsum(a, axis=-1)
    seg = a_cum[..., :, None] - a_cum[..., None, :]
    tri = jnp.tril(jnp.ones((CHUNK, CHUNK), dtype=bool))
    Lmat = jnp.exp(jnp.where(tri, seg, -jnp.inf))
    CB = jnp.einsum('bclgn,bcsgn->bgcls', Cc, Bc)
    y_diag = jnp.einsum('bgrcls,bcsgrp->bclgrp', CB[:, :, None] * Lmat, xdt)
    decay_states = jnp.exp(a_cum[..., -1:] - a_cum)
    states = jnp.einsum('bclgn,bgrcl,bclgrp->bcgrpn', Bc, decay_states, xdt)
    chunk_decay = jnp.exp(a_cum[..., -1])

    def step(carry, inp):
        s_c, d_c = inp
        return carry * d_c[..., None, None] + s_c, carry

    init = jnp.zeros((b, g, r, p, n), f32)
    _, prev = lax.scan(step, init, (jnp.moveaxis(states, 1, 0), jnp.moveaxis(chunk_decay, -1, 0)))
    prev = jnp.moveaxis(prev, 0, 1)
    y_off = jnp.einsum('bclgn,bcgrpn,bgrcl->bclgrp', Cc, prev, jnp.exp(a_cum))
    return (y_diag + y_off).reshape(b, L, h, p)


def mixer(h, li, rel_bias, w_in, lambda_q1, lambda_k1, lambda_q2, lambda_k2, subln_w,
          conv_w, conv_b, dt_bias_f, dt_bias_b, a_log_f, a_log_b, d_skip, ssm_norm_w,
          w_proj_attn, w_proj_ssm, w_out):
    b, S, _ = h.shape
    proj = h @ w_in
    q, k, v, z, xbc, dt_raw, gates = jnp.split(proj, SPLIT_POINTS, axis=-1)
    q = q.reshape(b, S, ATT_HEADS, 2, ATT_HEAD_DIM)
    k = k.reshape(b, S, ATT_HEADS, 2, ATT_HEAD_DIM)
    v = v.reshape(b, S, ATT_HEADS, ATT_V_DIM)
    lam_init = 0.8 - 0.6 * math.exp(-0.3 * li)
    f32 = jnp.float32
    lam = (jnp.exp(jnp.sum(lambda_q1.astype(f32) * lambda_k1.astype(f32)))
           - jnp.exp(jnp.sum(lambda_q2.astype(f32) * lambda_k2.astype(f32))) + lam_init)
    att = diff_attention(q, k, v, lam, rel_bias)
    att = rmsnorm(att, subln_w) * (1.0 - lam_init)
    att = att.reshape(b, S, ATT_WIDTH).astype(h.dtype)
    xbc = lax.conv_general_dilated(xbc, conv_w, window_strides=(1,),
                                   padding=[(D_CONV // 2, D_CONV // 2)],
                                   dimension_numbers=('NWC', 'WIO', 'NWC'),
                                   feature_group_count=CONV_CH)
    xbc = jax.nn.silu(xbc + conv_b)
    xs, Bm, Cm = jnp.split(xbc, (SSM_INNER, SSM_INNER + SSM_GROUPS * SSM_STATE), axis=-1)
    xs = xs.reshape(b, S, SSM_HEADS, SSM_HEAD_DIM)
    Bm = Bm.reshape(b, S, SSM_GROUPS, SSM_STATE)
    Cm = Cm.reshape(b, S, SSM_GROUPS, SSM_STATE)
    dt_f = jax.nn.softplus((dt_raw[..., :SSM_HEADS] + dt_bias_f).astype(f32))
    dt_b = jax.nn.softplus((dt_raw[..., SSM_HEADS:] + dt_bias_b).astype(f32))
    A_f = -jnp.exp(a_log_f.astype(f32))
    A_b = -jnp.exp(a_log_b.astype(f32))
    y_f = ssd_scan(xs, dt_f, A_f, Bm, Cm)
    y_b = jnp.flip(ssd_scan(jnp.flip(xs, 1), jnp.flip(dt_b, 1), A_b,
                            jnp.flip(Bm, 1), jnp.flip(Cm, 1)), 1)
    y = y_f + y_b + d_skip.astype(f32)[:, None] * xs.astype(f32)
    y = y.reshape(b, S, SSM_INNER) * jax.nn.silu(z.astype(f32))
    y_ssm = rmsnorm(y, ssm_norm_w).astype(h.dtype)
    g = jax.nn.sigmoid(gates.reshape(b, S, 2, D_MODEL))
    merged = g[:, :, 0] * (att @ w_proj_attn) + g[:, :, 1] * (y_ssm @ w_proj_ssm)
    return merged @ w_out


def trunk(x, rel_bias, norm_pre_mix, w_in, lambda_q1, lambda_k1, lambda_q2, lambda_k2,
          subln_w, conv_w, conv_b, dt_bias_f, dt_bias_b, a_log_f, a_log_b, d_skip,
          ssm_norm_w, w_proj_attn, w_proj_ssm, w_out, norm_post_mix, norm_pre_ffn,
          w_gate_up, w_down, norm_post_ffn):
    for l in range(DEPTH):
        h = rmsnorm(x, norm_pre_mix[l])
        m = mixer(h, l, rel_bias, w_in[l], lambda_q1[l], lambda_k1[l], lambda_q2[l],
                  lambda_k2[l], subln_w[l], conv_w[l], conv_b[l], dt_bias_f[l], dt_bias_b[l],
                  a_log_f[l], a_log_b[l], d_skip[l], ssm_norm_w[l], w_proj_attn[l],
                  w_proj_ssm[l], w_out[l])
        x = x + rmsnorm(m, norm_post_mix[l])
        h = rmsnorm(x, norm_pre_ffn[l])
        gate, up = jnp.split(h @ w_gate_up[l], 2, axis=-1)
        f = (jax.nn.silu(gate) * up) @ w_down[l]
        x = x + rmsnorm(f, norm_post_ffn[l])
    return x


def setup_inputs(seed: int = 0) -> dict:
    key = jax.random.key(seed)
    ks = jax.random.split(key, 32)
    f32 = jnp.float32

    def nrm(k, shape, scale):
        return jax.random.normal(k, shape, f32) * scale

    def gain(k, shape):
        return 1.0 + 0.05 * jax.random.normal(k, shape, f32)

    dt0 = jnp.exp(jax.random.uniform(ks[10], (2, DEPTH, SSM_HEADS), f32)
                  * (math.log(0.1) - math.log(0.001)) + math.log(0.001))
    dt_bias = dt0 + jnp.log(-jnp.expm1(-dt0))
    a_log = jnp.log(jax.random.uniform(ks[11], (2, DEPTH, SSM_HEADS), f32, 1.0, 16.0))
    return {
        'x_prompt': nrm(ks[0], (BATCH, SEQ, D_MODEL), 1.0),
        'x_sample': nrm(ks[1], (DEC_BATCH, DEC_SEQ, D_MODEL), 1.0),
        'rel_bias': nrm(ks[2], (NUM_BUCKETS, ATT_HEADS), 0.5),
        'norm_pre_mix': gain(ks[3], (DEPTH, D_MODEL)),
        'w_in': nrm(ks[4], (DEPTH, D_MODEL, IN_COLS), D_MODEL ** -0.5),
        'lambda_q1': nrm(ks[5], (DEPTH, ATT_HEAD_DIM), 0.1),
        'lambda_k1': nrm(ks[6], (DEPTH, ATT_HEAD_DIM), 0.1),
        'lambda_q2': nrm(ks[7], (DEPTH, ATT_HEAD_DIM), 0.1),
        'lambda_k2': nrm(ks[8], (DEPTH, ATT_HEAD_DIM), 0.1),
        'subln_w': gain(ks[9], (DEPTH, ATT_V_DIM)),
        'conv_w': nrm(ks[12], (DEPTH, D_CONV, 1, CONV_CH), D_CONV ** -0.5),
        'conv_b': nrm(ks[13], (DEPTH, CONV_CH), 0.01),
        'dt_bias_f': dt_bias[0],
        'dt_bias_b': dt_bias[1],
        'a_log_f': a_log[0],
        'a_log_b': a_log[1],
        'd_skip': gain(ks[14], (DEPTH, SSM_HEADS)),
        'ssm_norm_w': gain(ks[15], (DEPTH, SSM_INNER)),
        'w_proj_attn': nrm(ks[16], (DEPTH, ATT_WIDTH, D_MODEL), ATT_WIDTH ** -0.5),
        'w_proj_ssm': nrm(ks[17], (DEPTH, SSM_INNER, D_MODEL), SSM_INNER ** -0.5),
        'w_out': nrm(ks[18], (DEPTH, D_MODEL, D_MODEL), D_MODEL ** -0.5),
        'norm_post_mix': gain(ks[19], (DEPTH, D_MODEL)),
        'norm_pre_ffn': gain(ks[20], (DEPTH, D_MODEL)),
        'w_gate_up': nrm(ks[21], (DEPTH, D_MODEL, 2 * D_FF), D_MODEL ** -0.5),
        'w_down': nrm(ks[22], (DEPTH, D_FF, D_MODEL), D_FF ** -0.5),
        'norm_post_ffn': gain(ks[23], (DEPTH, D_MODEL)),
    }


def reference(x_prompt, x_sample, rel_bias, norm_pre_mix, w_in, lambda_q1, lambda_k1,
              lambda_q2, lambda_k2, subln_w, conv_w, conv_b, dt_bias_f, dt_bias_b,
              a_log_f, a_log_b, d_skip, ssm_norm_w, w_proj_attn, w_proj_ssm, w_out,
              norm_post_mix, norm_pre_ffn, w_gate_up, w_down, norm_post_ffn):
    y_prompt = trunk(x_prompt, rel_bias, norm_pre_mix, w_in, lambda_q1, lambda_k1, lambda_q2,
                     lambda_k2, subln_w, conv_w, conv_b, dt_bias_f, dt_bias_b, a_log_f,
                     a_log_b, d_skip, ssm_norm_w, w_proj_attn, w_proj_ssm, w_out,
                     norm_post_mix, norm_pre_ffn, w_gate_up, w_down, norm_post_ffn)
    y_sample = trunk(x_sample, rel_bias, norm_pre_mix, w_in, lambda_q1, lambda_k1, lambda_q2,
                     lambda_k2, subln_w, conv_w, conv_b, dt_bias_f, dt_bias_b, a_log_f,
                     a_log_b, d_skip, ssm_norm_w, w_proj_attn, w_proj_ssm, w_out,
                     norm_post_mix, norm_pre_ffn, w_gate_up, w_down, norm_post_ffn)
    return (y_prompt, y_sample)
```

```python
import functools
import math

import jax
import jax.numpy as jnp
import numpy as np
from jax import lax
from jax.experimental import pallas as pl
from jax.experimental.pallas import tpu as pltpu

F32 = jnp.float32
BF16 = jnp.bfloat16

D_MODEL = 1024
DEPTH = 2
ATT_HEADS = 8
ATT_HEAD_DIM = 64
ATT_V_DIM = 2 * ATT_HEAD_DIM
NUM_BUCKETS = 32
MAX_DISTANCE = 128
SSM_INNER = 1024
SSM_HEAD_DIM = 64
SSM_HEADS = 16
SSM_GROUPS = 2
SSM_STATE = 128
D_CONV = 5
CONV_CH = SSM_INNER + 2 * SSM_GROUPS * SSM_STATE
D_FF = 2816
EPS = 1e-6

LANES = 128
SUBLANES = 8
DT_PAD = LANES
VMEM_LIMIT = 56 * 1024 * 1024

SSD_CHUNK = 128
NBIAS = 5


def _cparams(sem):
    return pltpu.CompilerParams(dimension_semantics=sem, vmem_limit_bytes=VMEM_LIMIT)


def _rms(x, w):
    ms = jnp.mean(x * x, axis=-1, keepdims=True)
    return x * lax.rsqrt(ms + EPS) * w


def _sigmoid(x):
    return 1.0 / (1.0 + jnp.exp(-x))


def _resident(shape):
    zeros = (0,) * len(shape)
    return pl.BlockSpec(shape, lambda *_: zeros, pipeline_mode=pl.Buffered(1))


IN_SEGS = (D_MODEL, D_MODEL, D_MODEL, SSM_INNER, CONV_CH, DT_PAD, 2 * D_MODEL)
IN_DTYPES = (BF16, BF16, BF16, F32, F32, F32, F32)
IN_COLS_PAD = sum(IN_SEGS)
MM_COLS = 512


def _inproj_kernel(x_ref, nw_ref, w_ref, *out_refs):
    h = _rms(x_ref[...], nw_ref[...]).astype(BF16)
    off = 0
    for ref, n in zip(out_refs, IN_SEGS):
        for c0 in range(0, n, MM_COLS):
            cw = min(MM_COLS, n - c0)
            r = jnp.dot(h, w_ref[:, off + c0:off + c0 + cw], preferred_element_type=F32)
            ref[:, c0:c0 + cw] = r.astype(ref.dtype)
        off += n


def _inproj(x, nw, w, tm):
    t = x.shape[0]
    out_shape = tuple(jax.ShapeDtypeStruct((t, n), d) for n, d in zip(IN_SEGS, IN_DTYPES))
    out_specs = tuple(pl.BlockSpec((tm, n), lambda i: (i, 0)) for n in IN_SEGS)
    return pl.pallas_call(
        _inproj_kernel,
        grid=(t // tm,),
        in_specs=[pl.BlockSpec((tm, D_MODEL), lambda i: (i, 0)),
                  _resident((1, D_MODEL)),
                  _resident((D_MODEL, IN_COLS_PAD))],
        out_specs=out_specs,
        out_shape=out_shape,
        compiler_params=_cparams(("parallel",)),
        name="inproj",
    )(x, nw, w)


def _bias_kernel(tab_ref, ids_ref, o_ref):
    h = pl.program_id(0)
    ids = ids_ref[0]
    acc = jnp.zeros(ids.shape, F32)
    for n in range(NUM_BUCKETS):
        acc = jnp.where(ids == n, tab_ref[n, h], acc)
    o_ref[0, 0] = acc


def _bias_tiles(rel_bias, tq, tk):
    ids = _bucket_ids(tq, tk)
    return pl.pallas_call(
        _bias_kernel,
        grid=(ATT_HEADS, NBIAS),
        in_specs=[pl.BlockSpec(memory_space=pltpu.SMEM),
                  pl.BlockSpec((1, tq, tk), lambda h, d: (d, 0, 0))],
        out_specs=pl.BlockSpec((1, 1, tq, tk), lambda h, d: (h, d, 0, 0)),
        out_shape=jax.ShapeDtypeStruct((ATT_HEADS, NBIAS, tq, tk), F32),
        compiler_params=_cparams(("parallel", "parallel")),
        name="bias_tiles",
    )(rel_bias, ids)


def _bucket_ids(tq, tk):
    assert tq == tk and tq >= MAX_DISTANCE
    d = jnp.arange(-(NBIAS // 2), NBIAS // 2 + 1, dtype=jnp.int32)[:, None, None] * tk
    rel = d + jnp.arange(tk, dtype=jnp.int32)[None, None, :] - jnp.arange(tq, dtype=jnp.int32)[None, :, None]
    nb = NUM_BUCKETS // 2
    ret = (rel > 0).astype(jnp.int32) * nb
    n = jnp.abs(rel)
    max_exact = nb // 2
    nf = jnp.maximum(n, 1).astype(F32)
    large = max_exact + (jnp.log(nf / max_exact) / math.log(MAX_DISTANCE / max_exact)
                         * (nb - max_exact)).astype(jnp.int32)
    large = jnp.minimum(large, nb - 1)
    return ret + jnp.where(n < max_exact, n, large)


def _attn_kernel(q_ref, k_ref, v_ref, b_ref, lq1_ref, lk1_ref, lq2_ref, lk2_ref, sw_ref, o_ref,
                 m1_ref, l1_ref, a1_ref, m2_ref, l2_ref, a2_ref, *, tk, nk, lam_init):
    qi = pl.program_id(2)
    q = q_ref[...]
    lane = lax.broadcasted_iota(jnp.int32, q.shape, 1)
    zero = jnp.zeros_like(q)
    qs = (jnp.where(lane < ATT_HEAD_DIM, q, zero), jnp.where(lane >= ATT_HEAD_DIM, q, zero))
    stats = ((m1_ref, l1_ref, a1_ref), (m2_ref, l2_ref, a2_ref))
    for m_ref, l_ref, a_ref in stats:
        m_ref[...] = jnp.full(m_ref.shape, -jnp.inf, F32)
        l_ref[...] = jnp.zeros(l_ref.shape, F32)
        a_ref[...] = jnp.zeros(a_ref.shape, F32)

    def step(kj, carry):
        start = pl.multiple_of(kj * tk, tk)
        k = k_ref[pl.ds(start, tk), :]
        v = v_ref[pl.ds(start, tk), :]
        bias = b_ref[0, jnp.clip(kj - qi, -(NBIAS // 2), NBIAS // 2) + NBIAS // 2]
        for qm, (m_ref, l_ref, a_ref) in zip(qs, stats):
            s = lax.dot_general(qm, k, (((1,), (1,)), ((), ())), preferred_element_type=F32) + bias
            m_prev = m_ref[...]
            m_new = jnp.maximum(m_prev, jnp.max(s, axis=1, keepdims=True))
            alpha = jnp.exp(m_prev - m_new)
            p = jnp.exp(s - jnp.concatenate([m_new] * (tk // LANES), axis=1))
            l_ref[...] = alpha * l_ref[...] + jnp.sum(p, axis=1, keepdims=True)
            a_ref[...] = alpha * a_ref[...] + jnp.dot(p.astype(BF16), v, preferred_element_type=F32)
            m_ref[...] = m_new
        return carry

    lax.fori_loop(0, nk, step, 0)

    lam = (jnp.exp(jnp.sum(lq1_ref[...] * lk1_ref[...], axis=1, keepdims=True))
           - jnp.exp(jnp.sum(lq2_ref[...] * lk2_ref[...], axis=1, keepdims=True)) + lam_init)
    o = a1_ref[...] / l1_ref[...] - lam * (a2_ref[...] / l2_ref[...])
    o_ref[...] = (_rms(o, sw_ref[...]) * (1.0 - lam_init)).astype(o_ref.dtype)


def _attention(q, k, v, bias, lq1, lk1, lq2, lk2, subln_w, *, b, s, tq, lam_init):
    t = b * s
    nq = s // tq
    tk = tq
    kern = functools.partial(_attn_kernel, tk=tk, nk=s // tk, lam_init=lam_init)
    vec = pl.BlockSpec((1, ATT_HEAD_DIM), lambda bi, h, qi: (0, 0))
    stat = pltpu.VMEM((tq, LANES), F32)
    return pl.pallas_call(
        kern,
        grid=(b, ATT_HEADS, nq),
        in_specs=[pl.BlockSpec((tq, LANES), lambda bi, h, qi: (bi * nq + qi, h)),
                  pl.BlockSpec((s, LANES), lambda bi, h, qi: (bi, h)),
                  pl.BlockSpec((s, LANES), lambda bi, h, qi: (bi, h)),
                  pl.BlockSpec((1, NBIAS, tq, tk), lambda bi, h, qi: (h, 0, 0, 0)),
                  vec, vec, vec, vec,
                  pl.BlockSpec((1, ATT_V_DIM), lambda bi, h, qi: (0, 0))],
        out_specs=pl.BlockSpec((tq, LANES), lambda bi, h, qi: (bi * nq + qi, h)),
        out_shape=jax.ShapeDtypeStruct((t, ATT_HEADS * ATT_V_DIM), BF16),
        scratch_shapes=[stat] * 6,
        compiler_params=_cparams(("parallel", "parallel", "parallel")),
        name="diff_attn",
    )(q, k, v, bias, lq1, lk1, lq2, lk2, subln_w)


HALO = SUBLANES


def _conv_kernel(prev_ref, cur_ref, next_ref, w_ref, b_ref, o_ref, ext_ref, *, tc):
    i = pl.program_id(1)
    n = pl.num_programs(1)
    ext_ref[0:HALO, :] = jnp.where(i > 0, prev_ref[...], 0.0)
    ext_ref[HALO:HALO + tc, :] = cur_ref[...]
    ext_ref[HALO + tc:, :] = jnp.where(i < n - 1, next_ref[...], 0.0)
    acc = jnp.zeros((tc, CONV_CH), F32) + b_ref[...]
    for j in range(D_CONV):
        acc = acc + ext_ref[pl.ds(HALO - D_CONV // 2 + j, tc), :] * w_ref[j:j + 1, :]
    o_ref[...] = acc * _sigmoid(acc)


def _conv_silu(xbc, w, bias, *, b, s, tc):
    t = b * s
    nt = s // tc
    r = tc // HALO
    last = t // HALO - 1
    return pl.pallas_call(
        functools.partial(_conv_kernel, tc=tc),
        grid=(b, nt),
        in_specs=[pl.BlockSpec((HALO, CONV_CH), lambda bi, i: (jnp.maximum((bi * nt + i) * r - 1, 0), 0)),
                  pl.BlockSpec((tc, CONV_CH), lambda bi, i: (bi * nt + i, 0)),
                  pl.BlockSpec((HALO, CONV_CH), lambda bi, i: (jnp.minimum((bi * nt + i + 1) * r, last), 0)),
                  pl.BlockSpec((D_CONV, CONV_CH), lambda bi, i: (0, 0)),
                  pl.BlockSpec((1, CONV_CH), lambda bi, i: (0, 0))],
        out_specs=pl.BlockSpec((tc, CONV_CH), lambda bi, i: (bi * nt + i, 0)),
        out_shape=jax.ShapeDtypeStruct((t, CONV_CH), F32),
        scratch_shapes=[pltpu.VMEM((tc + 2 * HALO, CONV_CH), F32)],
        compiler_params=_cparams(("parallel", "parallel")),
        name="conv_silu",
    )(xbc, xbc, xbc, w, bias)


def _split3(x):
    hi = x.astype(BF16)
    r1 = x - hi.astype(F32)
    mid = r1.astype(BF16)
    lo = (r1 - mid.astype(F32)).astype(BF16)
    return hi, mid, lo


def _dot01_right(x, m01):
    return sum(jnp.dot(p, m01, preferred_element_type=F32) for p in _split3(x))


def _dot01_left(m01, x):
    return sum(jnp.dot(m01, p, preferred_element_type=F32) for p in _split3(x))


def _ssd_kernel(xc_ref, dt_ref, dtb_ref, alog_ref, tri_ref, e1_ref, e2_ref, y_ref, st_ref):
    d = pl.program_id(1)
    c = pl.program_id(2)
    L = SSD_CHUNK
    H = SSM_HEADS
    GW = SSM_INNER // SSM_GROUPS

    @pl.when(c == 0)
    def _():
        st_ref[...] = jnp.zeros(st_ref.shape, F32)

    fwd = d == 0
    dtr = dt_ref[...]
    dtr = jnp.where(fwd, dtr[:, 0:H], dtr[:, H:2 * H])
    dtb = dtb_ref[...]
    alog = alog_ref[...]
    xin = dtr + jnp.where(fwd, dtb[0:1, :], dtb[1:2, :])
    dt = jnp.maximum(xin, 0.0) + jnp.log(1.0 + jnp.exp(-jnp.abs(xin)))
    a = dt * (-jnp.exp(jnp.where(fwd, alog[0:1, :], alog[1:2, :])))

    tri_d = jnp.where(fwd, tri_ref[0], tri_ref[1])
    tri_t = jnp.where(fwd, tri_ref[1], tri_ref[0])
    mask = tri_d > 0
    u = _dot01_left(tri_d, a)
    u_t = _dot01_right(a.T, tri_t)
    tot = jnp.sum(a, axis=0, keepdims=True)

    e1 = e1_ref[...]
    e2 = e2_ref[...]
    dt_full = _dot01_right(dt, e1)
    dts_full = _dot01_right(dt * jnp.exp(tot - u), e1)
    osc_full = _dot01_right(jnp.exp(u), e1)
    dec_full = _dot01_right(jnp.broadcast_to(jnp.exp(tot), (SUBLANES, H)), e1)[0:1, :]
    u_b = _dot01_right(u, e2)

    xs = xc_ref[:, 0:SSM_INNER]
    xdt = (xs * dt_full).astype(BF16)
    xdts = (xs * dts_full).astype(BF16)
    lane = lax.broadcasted_iota(jnp.int32, (L, LANES), 1)
    left = lane < SSM_HEAD_DIM
    zero = jnp.zeros((L, LANES), BF16)
    hpg = H // SSM_GROUPS

    for g in range(SSM_GROUPS):
        bg = xc_ref[:, SSM_INNER + g * SSM_STATE:SSM_INNER + (g + 1) * SSM_STATE].astype(BF16)
        c0 = SSM_INNER + SSM_GROUPS * SSM_STATE + g * SSM_STATE
        cg = xc_ref[:, c0:c0 + SSM_STATE].astype(BF16)
        cb = lax.dot_general(cg, bg, (((1,), (1,)), ((), ())), preferred_element_type=F32)
        st = st_ref[:, g * GW:(g + 1) * GW]
        yoff = jnp.dot(cg, st.astype(BF16), preferred_element_type=F32)
        for j in range(hpg // 2):
            col = g * GW + j * LANES
            xp = xdt[:, col:col + LANES]
            acc = yoff[:, j * LANES:(j + 1) * LANES] * osc_full[:, col:col + LANES]
            for side in range(2):
                h = g * hpg + 2 * j + side
                seg = u_b[:, h * LANES:(h + 1) * LANES] - u_t[h:h + 1, :]
                lm = jnp.exp(jnp.where(mask, seg, -jnp.inf))
                mh = (cb * lm).astype(BF16)
                xh = jnp.where(left if side == 0 else jnp.logical_not(left), xp, zero)
                acc = acc + jnp.dot(mh, xh, preferred_element_type=F32)
            y_ref[0, :, col:col + LANES] = acc
        upd = lax.dot_general(bg, xdts[:, g * GW:(g + 1) * GW], (((0,), (0,)), ((), ())),
                              preferred_element_type=F32)
        st_ref[:, g * GW:(g + 1) * GW] = st * dec_full[:, g * GW:(g + 1) * GW] + upd


def _ssd(xconv, dt_raw, dt_bias, a_log, *, b, s):
    t = b * s
    L = SSD_CHUNK
    nc = s // L
    tri = np.tril(np.ones((L, L), np.float32))
    tri2 = jnp.asarray(np.stack([tri, tri.T]), BF16)
    heads = np.arange(SSM_HEADS)[:, None]
    e1 = jnp.asarray(np.arange(SSM_INNER)[None, :] // SSM_HEAD_DIM == heads, BF16)
    e2 = jnp.asarray(np.arange(SSM_HEADS * LANES)[None, :] // LANES == heads, BF16)

    def chunk(bi, d, c):
        return bi * nc + jnp.where(d == 0, c, nc - 1 - c)

    const2 = lambda bi, d, c: (0, 0)
    return pl.pallas_call(
        _ssd_kernel,
        grid=(b, 2, nc),
        in_specs=[pl.BlockSpec((L, CONV_CH), lambda bi, d, c: (chunk(bi, d, c), 0)),
                  pl.BlockSpec((L, DT_PAD), lambda bi, d, c: (chunk(bi, d, c), 0)),
                  pl.BlockSpec((2, SSM_HEADS), const2),
                  pl.BlockSpec((2, SSM_HEADS), const2),
                  pl.BlockSpec((2, L, L), lambda bi, d, c: (0, 0, 0)),
                  pl.BlockSpec((SSM_HEADS, SSM_INNER), const2),
                  pl.BlockSpec((SSM_HEADS, SSM_HEADS * LANES), const2)],
        out_specs=pl.BlockSpec((1, L, SSM_INNER), lambda bi, d, c: (d, chunk(bi, d, c), 0)),
        out_shape=jax.ShapeDtypeStruct((2, t, SSM_INNER), F32),
        scratch_shapes=[pltpu.VMEM((SSM_STATE, SSM_INNER), F32)],
        compiler_params=_cparams(("parallel", "parallel", "arbitrary")),
        name="ssd_scan",
    )(xconv, dt_raw, dt_bias, a_log, tri2, e1, e2)


def _merge_kernel(x_ref, yf_ref, yb_ref, xs_ref, z_ref, att_ref, g_ref, dsk_ref, snw_ref,
                  wpa_ref, wps_ref, wo_ref, pnw_ref, o_ref):
    z = z_ref[...]
    y = yf_ref[0] + yb_ref[0] + dsk_ref[...] * xs_ref[...]
    y = y * (z * _sigmoid(z))
    y_ssm = _rms(y, snw_ref[...]).astype(BF16)
    pa = jnp.dot(att_ref[...], wpa_ref[...], preferred_element_type=F32)
    ps = jnp.dot(y_ssm, wps_ref[...], preferred_element_type=F32)
    merged = _sigmoid(g_ref[:, 0:D_MODEL]) * pa + _sigmoid(g_ref[:, D_MODEL:2 * D_MODEL]) * ps
    m = jnp.dot(merged.astype(BF16), wo_ref[...], preferred_element_type=F32)
    o_ref[...] = x_ref[...] + _rms(m, pnw_ref[...])


def _merge(x, y2, xconv, z, att, gates, dsk, snw, wpa, wps, wo, pnw, tm):
    t = x.shape[0]
    row = lambda n: pl.BlockSpec((tm, n), lambda i: (i, 0))
    return pl.pallas_call(
        _merge_kernel,
        grid=(t // tm,),
        in_specs=[row(D_MODEL),
                  pl.BlockSpec((1, tm, SSM_INNER), lambda i: (0, i, 0)),
                  pl.BlockSpec((1, tm, SSM_INNER), lambda i: (1, i, 0)),
                  row(SSM_INNER),
                  row(SSM_INNER), row(ATT_HEADS * ATT_V_DIM), row(2 * D_MODEL),
                  _resident((1, SSM_INNER)), _resident((1, SSM_INNER)),
                  _resident((D_MODEL, D_MODEL)), _resident((D_MODEL, D_MODEL)),
                  _resident((D_MODEL, D_MODEL)), _resident((1, D_MODEL))],
        out_specs=row(D_MODEL),
        out_shape=jax.ShapeDtypeStruct((t, D_MODEL), F32),
        compiler_params=_cparams(("parallel",)),
        name="merge_proj",
    )(x, y2, y2, xconv, z, att, gates, dsk, snw, wpa, wps, wo, pnw)


FF_COLS = 256


def _ffn_kernel(x_ref, n1_ref, wgu_ref, wd_ref, n2_ref, o_ref, acc_ref):
    x = x_ref[...]
    h = _rms(x, n1_ref[...]).astype(BF16)
    for j in range(D_FF // FF_COLS):
        c0 = j * FF_COLS
        gate = jnp.dot(h, wgu_ref[:, c0:c0 + FF_COLS], preferred_element_type=F32)
        up = jnp.dot(h, wgu_ref[:, D_FF + c0:D_FF + c0 + FF_COLS], preferred_element_type=F32)
        act = (gate * _sigmoid(gate) * up).astype(BF16)
        part = jnp.dot(act, wd_ref[c0:c0 + FF_COLS, :], preferred_element_type=F32)
        if j == 0:
            acc_ref[...] = part
        else:
            acc_ref[...] += part
    o_ref[...] = x + _rms(acc_ref[...], n2_ref[...])


def _ffn(x, n1, wgu, wd, n2, tm):
    t = x.shape[0]
    return pl.pallas_call(
        _ffn_kernel,
        grid=(t // tm,),
        in_specs=[pl.BlockSpec((tm, D_MODEL), lambda i: (i, 0)),
                  _resident((1, D_MODEL)),
                  _resident((D_MODEL, 2 * D_FF)),
                  _resident((D_FF, D_MODEL)),
                  _resident((1, D_MODEL))],
        out_specs=pl.BlockSpec((tm, D_MODEL), lambda i: (i, 0)),
        out_shape=jax.ShapeDtypeStruct((t, D_MODEL), F32),
        scratch_shapes=[pltpu.VMEM((tm, D_MODEL), F32)],
        compiler_params=_cparams(("parallel",)),
        name="swiglu_ffn",
    )(x, n1, wgu, wd, n2)


def _tiles(b, s):
    t = b * s
    return dict(tm=min(512, t), tq=min(256, s), tc=min(512, s))


def _prep_layer(l, p):
    w_in = p['w_in'][l]
    q_cols = ATT_HEADS * 2 * ATT_HEAD_DIM
    dt0 = 3 * q_cols + SSM_INNER + CONV_CH
    scale = ATT_HEAD_DIM ** -0.5
    w_pad = jnp.concatenate(
        [w_in[:, :q_cols] * scale, w_in[:, q_cols:dt0], w_in[:, dt0:dt0 + 2 * SSM_HEADS],
         jnp.zeros((D_MODEL, DT_PAD - 2 * SSM_HEADS), F32), w_in[:, dt0 + 2 * SSM_HEADS:]], axis=1)
    row = lambda v: v.reshape(1, -1)
    return dict(
        norm_pre_mix=row(p['norm_pre_mix'][l]), w_in=w_pad.astype(BF16),
        lq1=row(p['lambda_q1'][l]), lk1=row(p['lambda_k1'][l]),
        lq2=row(p['lambda_q2'][l]), lk2=row(p['lambda_k2'][l]),
        subln_w=row(p['subln_w'][l]),
        conv_w=p['conv_w'][l].reshape(D_CONV, CONV_CH), conv_b=row(p['conv_b'][l]),
        dt_bias=jnp.stack([p['dt_bias_f'][l], p['dt_bias_b'][l]]),
        a_log=jnp.stack([p['a_log_f'][l], p['a_log_b'][l]]),
        d_skip=row(jnp.repeat(p['d_skip'][l], SSM_HEAD_DIM)),
        ssm_norm_w=row(p['ssm_norm_w'][l]),
        w_proj_attn=p['w_proj_attn'][l].astype(BF16), w_proj_ssm=p['w_proj_ssm'][l].astype(BF16),
        w_out=p['w_out'][l].astype(BF16), norm_post_mix=row(p['norm_post_mix'][l]),
        norm_pre_ffn=row(p['norm_pre_ffn'][l]), w_gate_up=p['w_gate_up'][l].astype(BF16),
        w_down=p['w_down'][l].astype(BF16), norm_post_ffn=row(p['norm_post_ffn'][l]))


def _trunk(x3, layers, bias_for):
    b, s, _ = x3.shape
    tl = _tiles(b, s)
    x = x3.reshape(b * s, D_MODEL)
    bias = bias_for(tl['tq'])
    for l, w in enumerate(layers):
        lam_init = 0.8 - 0.6 * math.exp(-0.3 * l)
        q, k, v, z, xbc, dt_raw, gates = _inproj(x, w['norm_pre_mix'], w['w_in'], tl['tm'])
        att = _attention(q, k, v, bias, w['lq1'], w['lk1'], w['lq2'], w['lk2'], w['subln_w'],
                         b=b, s=s, tq=tl['tq'], lam_init=lam_init)
        xconv = _conv_silu(xbc, w['conv_w'], w['conv_b'], b=b, s=s, tc=tl['tc'])
        y2 = _ssd(xconv, dt_raw, w['dt_bias'], w['a_log'], b=b, s=s)
        x = _merge(x, y2, xconv, z, att, gates, w['d_skip'], w['ssm_norm_w'], w['w_proj_attn'],
                   w['w_proj_ssm'], w['w_out'], w['norm_post_mix'], tl['tm'])
        x = _ffn(x, w['norm_pre_ffn'], w['w_gate_up'], w['w_down'], w['norm_post_ffn'], tl['tm'])
    return x.reshape(b, s, D_MODEL)


def kernel(x_prompt, x_sample, rel_bias, norm_pre_mix, w_in, lambda_q1, lambda_k1, lambda_q2,
           lambda_k2, subln_w, conv_w, conv_b, dt_bias_f, dt_bias_b, a_log_f, a_log_b, d_skip,
           ssm_norm_w, w_proj_attn, w_proj_ssm, w_out, norm_post_mix, norm_pre_ffn, w_gate_up,
           w_down, norm_post_ffn):
    p = dict(norm_pre_mix=norm_pre_mix, w_in=w_in, lambda_q1=lambda_q1, lambda_k1=lambda_k1,
             lambda_q2=lambda_q2, lambda_k2=lambda_k2, subln_w=subln_w, conv_w=conv_w,
             conv_b=conv_b, dt_bias_f=dt_bias_f, dt_bias_b=dt_bias_b, a_log_f=a_log_f,
             a_log_b=a_log_b, d_skip=d_skip, ssm_norm_w=ssm_norm_w, w_proj_attn=w_proj_attn,
             w_proj_ssm=w_proj_ssm, w_out=w_out, norm_post_mix=norm_post_mix,
             norm_pre_ffn=norm_pre_ffn, w_gate_up=w_gate_up, w_down=w_down,
             norm_post_ffn=norm_post_ffn)
    layers = [_prep_layer(l, p) for l in range(DEPTH)]
    cache = {}

    def bias_for(tq):
        if tq not in cache:
            cache[tq] = _bias_tiles(rel_bias, tq, tq)
        return cache[tq]

    return (_trunk(x_prompt, layers, bias_for), _trunk(x_sample, layers, bias_for))
```

```python
import functools
import math

import jax
import jax.numpy as jnp
import numpy as np
from jax import lax
from jax.experimental import pallas as pl
from jax.experimental.pallas import tpu as pltpu

F32 = jnp.float32
BF16 = jnp.bfloat16

D_MODEL = 1024
DEPTH = 2
ATT_HEADS = 8
ATT_HEAD_DIM = 64
ATT_V_DIM = 2 * ATT_HEAD_DIM
NUM_BUCKETS = 32
MAX_DISTANCE = 128
SSM_INNER = 1024
SSM_HEAD_DIM = 64
SSM_HEADS = 16
SSM_GROUPS = 2
SSM_STATE = 128
D_CONV = 5
CONV_CH = SSM_INNER + 2 * SSM_GROUPS * SSM_STATE
D_FF = 2816
EPS = 1e-6

LANES = 128
SUBLANES = 8
DT_PAD = LANES
VMEM_LIMIT = 56 * 1024 * 1024

SSD_CHUNK = 128


def _cparams(sem):
    return pltpu.CompilerParams(dimension_semantics=sem, vmem_limit_bytes=VMEM_LIMIT)


def _rms(x, w):
    ms = jnp.mean(x * x, axis=-1, keepdims=True)
    return x * lax.rsqrt(ms + EPS) * w


def _sigmoid(x):
    return 1.0 / (1.0 + jnp.exp(-x))


def _resident(shape):
    zeros = (0,) * len(shape)
    return pl.BlockSpec(shape, lambda *_: zeros, pipeline_mode=pl.Buffered(1))


IN_SEGS = (D_MODEL, D_MODEL, D_MODEL, SSM_INNER, CONV_CH, DT_PAD, 2 * D_MODEL)
IN_DTYPES = (BF16, BF16, BF16, F32, F32, F32, F32)
IN_COLS_PAD = sum(IN_SEGS)
MM_COLS = 512


def _inproj_kernel(x_ref, nw_ref, w_ref, *out_refs):
    h = _rms(x_ref[...], nw_ref[...]).astype(BF16)
    off = 0
    for ref, n in zip(out_refs, IN_SEGS):
        for c0 in range(0, n, MM_COLS):
            cw = min(MM_COLS, n - c0)
            r = jnp.dot(h, w_ref[:, off + c0:off + c0 + cw], preferred_element_type=F32)
            ref[:, c0:c0 + cw] = r.astype(ref.dtype)
        off += n


def _inproj(x, nw, w, tm):
    t = x.shape[0]
    out_shape = tuple(jax.ShapeDtypeStruct((t, n), d) for n, d in zip(IN_SEGS, IN_DTYPES))
    out_specs = tuple(pl.BlockSpec((tm, n), lambda i: (i, 0)) for n in IN_SEGS)
    return pl.pallas_call(
        _inproj_kernel,
        grid=(t // tm,),
        in_specs=[pl.BlockSpec((tm, D_MODEL), lambda i: (i, 0)),
                  _resident((1, D_MODEL)),
                  _resident((D_MODEL, IN_COLS_PAD))],
        out_specs=out_specs,
        out_shape=out_shape,
        compiler_params=_cparams(("parallel",)),
        name="inproj",
    )(x, nw, w)


LOG2E = math.log2(math.e)


def _bias_kernel(tab_ref, ids_ref, o_ref):
    h = pl.program_id(0)
    ids = ids_ref[0]
    acc = jnp.zeros(ids.shape, F32)
    for n in range(NUM_BUCKETS):
        acc = jnp.where(ids == n, tab_ref[n, h], acc)
    o_ref[0, 0] = acc * LOG2E


def _bias_offsets(tq, tk):
    assert tk % tq == 0 and tq >= MAX_DISTANCE
    return -(tk // tq + 1), 2


def _bias_tiles(rel_bias, tq, tk):
    ids = _bucket_ids(tq, tk)
    nbias = ids.shape[0]
    return pl.pallas_call(
        _bias_kernel,
        grid=(ATT_HEADS, nbias),
        in_specs=[pl.BlockSpec(memory_space=pltpu.SMEM),
                  pl.BlockSpec((1, tq, tk), lambda h, d: (d, 0, 0))],
        out_specs=pl.BlockSpec((1, 1, tq, tk), lambda h, d: (h, d, 0, 0)),
        out_shape=jax.ShapeDtypeStruct((ATT_HEADS, nbias, tq, tk), F32),
        compiler_params=_cparams(("parallel", "parallel")),
        name="bias_tiles",
    )(rel_bias, ids)


def _bucket_ids(tq, tk):
    lo, hi = _bias_offsets(tq, tk)
    d = jnp.arange(lo, hi + 1, dtype=jnp.int32)[:, None, None] * tq
    rel = d + jnp.arange(tk, dtype=jnp.int32)[None, None, :] - jnp.arange(tq, dtype=jnp.int32)[None, :, None]
    nb = NUM_BUCKETS // 2
    ret = (rel > 0).astype(jnp.int32) * nb
    n = jnp.abs(rel)
    max_exact = nb // 2
    nf = jnp.maximum(n, 1).astype(F32)
    large = max_exact + (jnp.log(nf / max_exact) / math.log(MAX_DISTANCE / max_exact)
                         * (nb - max_exact)).astype(jnp.int32)
    large = jnp.minimum(large, nb - 1)
    return ret + jnp.where(n < max_exact, n, large)


def _attn_kernel(q_ref, k_ref, v_ref, b_ref, lq1_ref, lk1_ref, lq2_ref, lk2_ref, sw_ref, o_ref,
                 s_ref, m_ref, acc_ref, vext_ref, *, tq, tk, nk, lam_init):
    qi = pl.program_id(2)
    lo, hi = _bias_offsets(tq, tk)
    nfold = tk // LANES

    @pl.when(qi == 0)
    def _():
        vext_ref[:, 0:ATT_V_DIM] = v_ref[...]
        vext_ref[:, ATT_V_DIM:] = jnp.ones((vext_ref.shape[0], LANES), BF16)

    q = q_ref[...]
    lane = lax.broadcasted_iota(jnp.int32, q.shape, 1)
    zero = jnp.zeros_like(q)
    qs = (jnp.where(lane < ATT_HEAD_DIM, q, zero), jnp.where(lane >= ATT_HEAD_DIM, q, zero))

    for mp in range(2):
        fold = None
        for kj in range(nk):
            k = k_ref[kj * tk:(kj + 1) * tk, :]
            bias = b_ref[0, jnp.clip((tk // tq) * kj - qi, lo, hi) - lo]
            s = lax.dot_general(qs[mp], k, (((1,), (1,)), ((), ())), preferred_element_type=F32) + bias
            s_ref[mp, kj] = s
            for c in range(nfold):
                part = s[:, c * LANES:(c + 1) * LANES]
                fold = part if fold is None else jnp.maximum(fold, part)
        m_ref[mp] = jnp.broadcast_to(jnp.max(fold, axis=1, keepdims=True), (tq, LANES))

    for mp in range(2):
        acc = None
        for kj in range(nk):
            m = m_ref[mp]
            p = jnp.exp2(s_ref[mp, kj] - jnp.concatenate([m] * nfold, axis=1))
            pv = jnp.dot(p.astype(BF16), vext_ref[kj * tk:(kj + 1) * tk, :], preferred_element_type=F32)
            acc = pv if acc is None else acc + pv
        acc_ref[mp] = acc

    lam = (jnp.exp(jnp.sum(lq1_ref[...] * lk1_ref[...], axis=1, keepdims=True))
           - jnp.exp(jnp.sum(lq2_ref[...] * lk2_ref[...], axis=1, keepdims=True)) + lam_init)
    o = (acc_ref[0, :, 0:ATT_V_DIM] / acc_ref[0, :, ATT_V_DIM:]
         - lam * (acc_ref[1, :, 0:ATT_V_DIM] / acc_ref[1, :, ATT_V_DIM:]))
    o_ref[...] = (_rms(o, sw_ref[...]) * (1.0 - lam_init)).astype(o_ref.dtype)


def _attention(q, k, v, bias, lq1, lk1, lq2, lk2, subln_w, *, b, s, tq, tk, lam_init):
    t = b * s
    nq = s // tq
    nk = s // tk
    nbias = bias.shape[1]
    kern = functools.partial(_attn_kernel, tq=tq, tk=tk, nk=nk, lam_init=lam_init)
    vec = pl.BlockSpec((1, ATT_HEAD_DIM), lambda bi, h, qi: (0, 0))
    return pl.pallas_call(
        kern,
        grid=(b, ATT_HEADS, nq),
        in_specs=[pl.BlockSpec((tq, LANES), lambda bi, h, qi: (bi * nq + qi, h)),
                  pl.BlockSpec((s, LANES), lambda bi, h, qi: (bi, h)),
                  pl.BlockSpec((s, LANES), lambda bi, h, qi: (bi, h)),
                  pl.BlockSpec((1, nbias, tq, tk), lambda bi, h, qi: (h, 0, 0, 0)),
                  vec, vec, vec, vec,
                  pl.BlockSpec((1, ATT_V_DIM), lambda bi, h, qi: (0, 0))],
        out_specs=pl.BlockSpec((tq, LANES), lambda bi, h, qi: (bi * nq + qi, h)),
        out_shape=jax.ShapeDtypeStruct((t, ATT_HEADS * ATT_V_DIM), BF16),
        scratch_shapes=[pltpu.VMEM((2, nk, tq, tk), F32),
                        pltpu.VMEM((2, tq, LANES), F32),
                        pltpu.VMEM((2, tq, ATT_V_DIM + LANES), F32),
                        pltpu.VMEM((s, ATT_V_DIM + LANES), BF16)],
        compiler_params=_cparams(("parallel", "parallel", "arbitrary")),
        name="diff_attn",
    )(q, k, v, bias, lq1, lk1, lq2, lk2, subln_w)


HALO = SUBLANES


def _conv_kernel(prev_ref, cur_ref, next_ref, w_ref, b_ref, o_ref, ext_ref, *, tc):
    i = pl.program_id(1)
    n = pl.num_programs(1)
    ext_ref[0:HALO, :] = jnp.where(i > 0, prev_ref[...], 0.0)
    ext_ref[HALO:HALO + tc, :] = cur_ref[...]
    ext_ref[HALO + tc:, :] = jnp.where(i < n - 1, next_ref[...], 0.0)
    acc = jnp.zeros((tc, CONV_CH), F32) + b_ref[...]
    for j in range(D_CONV):
        acc = acc + ext_ref[pl.ds(HALO - D_CONV // 2 + j, tc), :] * w_ref[j:j + 1, :]
    o_ref[...] = acc * _sigmoid(acc)


def _conv_silu(xbc, w, bias, *, b, s, tc):
    t = b * s
    nt = s // tc
    r = tc // HALO
    last = t // HALO - 1
    return pl.pallas_call(
        functools.partial(_conv_kernel, tc=tc),
        grid=(b, nt),
        in_specs=[pl.BlockSpec((HALO, CONV_CH), lambda bi, i: (jnp.maximum((bi * nt + i) * r - 1, 0), 0)),
                  pl.BlockSpec((tc, CONV_CH), lambda bi, i: (bi * nt + i, 0)),
                  pl.BlockSpec((HALO, CONV_CH), lambda bi, i: (jnp.minimum((bi * nt + i + 1) * r, last), 0)),
                  pl.BlockSpec((D_CONV, CONV_CH), lambda bi, i: (0, 0)),
                  pl.BlockSpec((1, CONV_CH), lambda bi, i: (0, 0))],
        out_specs=pl.BlockSpec((tc, CONV_CH), lambda bi, i: (bi * nt + i, 0)),
        out_shape=jax.ShapeDtypeStruct((t, CONV_CH), F32),
        scratch_shapes=[pltpu.VMEM((tc + 2 * HALO, CONV_CH), F32)],
        compiler_params=_cparams(("parallel", "parallel")),
        name="conv_silu",
    )(xbc, xbc, xbc, w, bias)


def _split3(x):
    hi = x.astype(BF16)
    r1 = x - hi.astype(F32)
    mid = r1.astype(BF16)
    lo = (r1 - mid.astype(F32)).astype(BF16)
    return hi, mid, lo


def _dot01_right(x, m01):
    return sum(jnp.dot(p, m01, preferred_element_type=F32) for p in _split3(x))


def _dot01_left(m01, x):
    return sum(jnp.dot(m01, p, preferred_element_type=F32) for p in _split3(x))


def _ssd_kernel(xc_ref, dt_ref, dtb_ref, alog_ref, tri_ref, e1_ref, e2_ref, y_ref, st_ref):
    d = pl.program_id(1)
    c = pl.program_id(2)
    L = SSD_CHUNK
    H = SSM_HEADS
    GW = SSM_INNER // SSM_GROUPS

    @pl.when(c == 0)
    def _():
        st_ref[...] = jnp.zeros(st_ref.shape, F32)

    fwd = d == 0
    dtr = dt_ref[...]
    dtr = jnp.where(fwd, dtr[:, 0:H], dtr[:, H:2 * H])
    dtb = dtb_ref[...]
    alog = alog_ref[...]
    xin = dtr + jnp.where(fwd, dtb[0:1, :], dtb[1:2, :])
    dt = jnp.maximum(xin, 0.0) + jnp.log(1.0 + jnp.exp(-jnp.abs(xin)))
    a = dt * (-jnp.exp(jnp.where(fwd, alog[0:1, :], alog[1:2, :])))

    tri_d = jnp.where(fwd, tri_ref[0], tri_ref[1])
    tri_t = jnp.where(fwd, tri_ref[1], tri_ref[0])
    mask = tri_d > 0
    u = _dot01_left(tri_d, a)
    u_t = _dot01_right(a.T, tri_t)
    tot = jnp.sum(a, axis=0, keepdims=True)

    e1 = e1_ref[...]
    e2 = e2_ref[...]
    dt_full = _dot01_right(dt, e1)
    dts_full = _dot01_right(dt * jnp.exp(tot - u), e1)
    osc_full = _dot01_right(jnp.exp(u), e1)
    dec_full = _dot01_right(jnp.broadcast_to(jnp.exp(tot), (SUBLANES, H)), e1)[0:1, :]
    u_b = _dot01_right(u, e2)

    xs = xc_ref[:, 0:SSM_INNER]
    xdt = (xs * dt_full).astype(BF16)
    xdts = (xs * dts_full).astype(BF16)
    lane = lax.broadcasted_iota(jnp.int32, (L, LANES), 1)
    left = lane < SSM_HEAD_DIM
    zero = jnp.zeros((L, LANES), BF16)
    hpg = H // SSM_GROUPS

    for g in range(SSM_GROUPS):
        bg = xc_ref[:, SSM_INNER + g * SSM_STATE:SSM_INNER + (g + 1) * SSM_STATE].astype(BF16)
        c0 = SSM_INNER + SSM_GROUPS * SSM_STATE + g * SSM_STATE
        cg = xc_ref[:, c0:c0 + SSM_STATE].astype(BF16)
        cb = lax.dot_general(cg, bg, (((1,), (1,)), ((), ())), preferred_element_type=F32)
        st = st_ref[:, g * GW:(g + 1) * GW]
        yoff = jnp.dot(cg, st.astype(BF16), preferred_element_type=F32)
        for j in range(hpg // 2):
            col = g * GW + j * LANES
            xp = xdt[:, col:col + LANES]
            acc = yoff[:, j * LANES:(j + 1) * LANES] * osc_full[:, col:col + LANES]
            for side in range(2):
                h = g * hpg + 2 * j + side
                seg = u_b[:, h * LANES:(h + 1) * LANES] - u_t[h:h + 1, :]
                lm = jnp.exp(jnp.where(mask, seg, -jnp.inf))
                mh = (cb * lm).astype(BF16)
                xh = jnp.where(left if side == 0 else jnp.logical_not(left), xp, zero)
                acc = acc + jnp.dot(mh, xh, preferred_element_type=F32)
            y_ref[0, :, col:col + LANES] = acc
        upd = lax.dot_general(bg, xdts[:, g * GW:(g + 1) * GW], (((0,), (0,)), ((), ())),
                              preferred_element_type=F32)
        st_ref[:, g * GW:(g + 1) * GW] = st * dec_full[:, g * GW:(g + 1) * GW] + upd


def _ssd(xconv, dt_raw, dt_bias, a_log, *, b, s):
    t = b * s
    L = SSD_CHUNK
    nc = s // L
    tri = np.tril(np.ones((L, L), np.float32))
    tri2 = jnp.asarray(np.stack([tri, tri.T]), BF16)
    heads = np.arange(SSM_HEADS)[:, None]
    e1 = jnp.asarray(np.arange(SSM_INNER)[None, :] // SSM_HEAD_DIM == heads, BF16)
    e2 = jnp.asarray(np.arange(SSM_HEADS * LANES)[None, :] // LANES == heads, BF16)

    def chunk(bi, d, c):
        return bi * nc + jnp.where(d == 0, c, nc - 1 - c)

    const2 = lambda bi, d, c: (0, 0)
    return pl.pallas_call(
        _ssd_kernel,
        grid=(b, 2, nc),
        in_specs=[pl.BlockSpec((L, CONV_CH), lambda bi, d, c: (chunk(bi, d, c), 0)),
                  pl.BlockSpec((L, DT_PAD), lambda bi, d, c: (chunk(bi, d, c), 0)),
                  pl.BlockSpec((2, SSM_HEADS), const2),
                  pl.BlockSpec((2, SSM_HEADS), const2),
                  pl.BlockSpec((2, L, L), lambda bi, d, c: (0, 0, 0)),
                  pl.BlockSpec((SSM_HEADS, SSM_INNER), const2),
                  pl.BlockSpec((SSM_HEADS, SSM_HEADS * LANES), const2)],
        out_specs=pl.BlockSpec((1, L, SSM_INNER), lambda bi, d, c: (d, chunk(bi, d, c), 0)),
        out_shape=jax.ShapeDtypeStruct((2, t, SSM_INNER), F32),
        scratch_shapes=[pltpu.VMEM((SSM_STATE, SSM_INNER), F32)],
        compiler_params=_cparams(("parallel", "parallel", "arbitrary")),
        name="ssd_scan",
    )(xconv, dt_raw, dt_bias, a_log, tri2, e1, e2)


def _merge_kernel(x_ref, yf_ref, yb_ref, xs_ref, z_ref, att_ref, g_ref, dsk_ref, snw_ref,
                  wpa_ref, wps_ref, wo_ref, pnw_ref, o_ref):
    z = z_ref[...]
    y = yf_ref[0] + yb_ref[0] + dsk_ref[...] * xs_ref[...]
    y = y * (z * _sigmoid(z))
    y_ssm = _rms(y, snw_ref[...]).astype(BF16)
    pa = jnp.dot(att_ref[...], wpa_ref[...], preferred_element_type=F32)
    ps = jnp.dot(y_ssm, wps_ref[...], preferred_element_type=F32)
    merged = _sigmoid(g_ref[:, 0:D_MODEL]) * pa + _sigmoid(g_ref[:, D_MODEL:2 * D_MODEL]) * ps
    m = jnp.dot(merged.astype(BF16), wo_ref[...], preferred_element_type=F32)
    o_ref[...] = x_ref[...] + _rms(m, pnw_ref[...])


def _merge(x, y2, xconv, z, att, gates, dsk, snw, wpa, wps, wo, pnw, tm):
    t = x.shape[0]
    row = lambda n: pl.BlockSpec((tm, n), lambda i: (i, 0))
    return pl.pallas_call(
        _merge_kernel,
        grid=(t // tm,),
        in_specs=[row(D_MODEL),
                  pl.BlockSpec((1, tm, SSM_INNER), lambda i: (0, i, 0)),
                  pl.BlockSpec((1, tm, SSM_INNER), lambda i: (1, i, 0)),
                  row(SSM_INNER),
                  row(SSM_INNER), row(ATT_HEADS * ATT_V_DIM), row(2 * D_MODEL),
                  _resident((1, SSM_INNER)), _resident((1, SSM_INNER)),
                  _resident((D_MODEL, D_MODEL)), _resident((D_MODEL, D_MODEL)),
                  _resident((D_MODEL, D_MODEL)), _resident((1, D_MODEL))],
        out_specs=row(D_MODEL),
        out_shape=jax.ShapeDtypeStruct((t, D_MODEL), F32),
        compiler_params=_cparams(("parallel",)),
        name="merge_proj",
    )(x, y2, y2, xconv, z, att, gates, dsk, snw, wpa, wps, wo, pnw)


FF_COLS = 256


def _ffn_kernel(x_ref, n1_ref, wgu_ref, wd_ref, n2_ref, o_ref, acc_ref):
    x = x_ref[...]
    h = _rms(x, n1_ref[...]).astype(BF16)
    for j in range(D_FF // FF_COLS):
        c0 = j * FF_COLS
        gate = jnp.dot(h, wgu_ref[:, c0:c0 + FF_COLS], preferred_element_type=F32)
        up = jnp.dot(h, wgu_ref[:, D_FF + c0:D_FF + c0 + FF_COLS], preferred_element_type=F32)
        act = (gate * _sigmoid(gate) * up).astype(BF16)
        part = jnp.dot(act, wd_ref[c0:c0 + FF_COLS, :], preferred_element_type=F32)
        if j == 0:
            acc_ref[...] = part
        else:
            acc_ref[...] += part
    o_ref[...] = x + _rms(acc_ref[...], n2_ref[...])


def _ffn(x, n1, wgu, wd, n2, tm):
    t = x.shape[0]
    return pl.pallas_call(
        _ffn_kernel,
        grid=(t // tm,),
        in_specs=[pl.BlockSpec((tm, D_MODEL), lambda i: (i, 0)),
                  _resident((1, D_MODEL)),
                  _resident((D_MODEL, 2 * D_FF)),
                  _resident((D_FF, D_MODEL)),
                  _resident((1, D_MODEL))],
        out_specs=pl.BlockSpec((tm, D_MODEL), lambda i: (i, 0)),
        out_shape=jax.ShapeDtypeStruct((t, D_MODEL), F32),
        scratch_shapes=[pltpu.VMEM((tm, D_MODEL), F32)],
        compiler_params=_cparams(("parallel",)),
        name="swiglu_ffn",
    )(x, n1, wgu, wd, n2)


def _tiles(b, s):
    t = b * s
    return dict(tm=min(512, t), tq=min(512, s), tk=min(512, s), tc=min(512, s))


def _prep_layer(l, p):
    w_in = p['w_in'][l]
    q_cols = ATT_HEADS * 2 * ATT_HEAD_DIM
    dt0 = 3 * q_cols + SSM_INNER + CONV_CH
    scale = ATT_HEAD_DIM ** -0.5 * LOG2E
    w_pad = jnp.concatenate(
        [w_in[:, :q_cols] * scale, w_in[:, q_cols:dt0], w_in[:, dt0:dt0 + 2 * SSM_HEADS],
         jnp.zeros((D_MODEL, DT_PAD - 2 * SSM_HEADS), F32), w_in[:, dt0 + 2 * SSM_HEADS:]], axis=1)
    row = lambda v: v.reshape(1, -1)
    return dict(
        norm_pre_mix=row(p['norm_pre_mix'][l]), w_in=w_pad.astype(BF16),
        lq1=row(p['lambda_q1'][l]), lk1=row(p['lambda_k1'][l]),
        lq2=row(p['lambda_q2'][l]), lk2=row(p['lambda_k2'][l]),
        subln_w=row(p['subln_w'][l]),
        conv_w=p['conv_w'][l].reshape(D_CONV, CONV_CH), conv_b=row(p['conv_b'][l]),
        dt_bias=jnp.stack([p['dt_bias_f'][l], p['dt_bias_b'][l]]),
        a_log=jnp.stack([p['a_log_f'][l], p['a_log_b'][l]]),
        d_skip=row(jnp.repeat(p['d_skip'][l], SSM_HEAD_DIM)),
        ssm_norm_w=row(p['ssm_norm_w'][l]),
        w_proj_attn=p['w_proj_attn'][l].astype(BF16), w_proj_ssm=p['w_proj_ssm'][l].astype(BF16),
        w_out=p['w_out'][l].astype(BF16), norm_post_mix=row(p['norm_post_mix'][l]),
        norm_pre_ffn=row(p['norm_pre_ffn'][l]), w_gate_up=p['w_gate_up'][l].astype(BF16),
        w_down=p['w_down'][l].astype(BF16), norm_post_ffn=row(p['norm_post_ffn'][l]))


def _trunk(x3, layers, bias_for):
    b, s, _ = x3.shape
    tl = _tiles(b, s)
    x = x3.reshape(b * s, D_MODEL)
    bias = bias_for(tl['tq'], tl['tk'])
    for l, w in enumerate(layers):
        lam_init = 0.8 - 0.6 * math.exp(-0.3 * l)
        q, k, v, z, xbc, dt_raw, gates = _inproj(x, w['norm_pre_mix'], w['w_in'], tl['tm'])
        att = _attention(q, k, v, bias, w['lq1'], w['lk1'], w['lq2'], w['lk2'], w['subln_w'],
                         b=b, s=s, tq=tl['tq'], tk=tl['tk'], lam_init=lam_init)
        xconv = _conv_silu(xbc, w['conv_w'], w['conv_b'], b=b, s=s, tc=tl['tc'])
        y2 = _ssd(xconv, dt_raw, w['dt_bias'], w['a_log'], b=b, s=s)
        x = _merge(x, y2, xconv, z, att, gates, w['d_skip'], w['ssm_norm_w'], w['w_proj_attn'],
                   w['w_proj_ssm'], w['w_out'], w['norm_post_mix'], tl['tm'])
        x = _ffn(x, w['norm_pre_ffn'], w['w_gate_up'], w['w_down'], w['norm_post_ffn'], tl['tm'])
    return x.reshape(b, s, D_MODEL)


def kernel(x_prompt, x_sample, rel_bias, norm_pre_mix, w_in, lambda_q1, lambda_k1, lambda_q2,
           lambda_k2, subln_w, conv_w, conv_b, dt_bias_f, dt_bias_b, a_log_f, a_log_b, d_skip,
           ssm_norm_w, w_proj_attn, w_proj_ssm, w_out, norm_post_mix, norm_pre_ffn, w_gate_up,
           w_down, norm_post_ffn):
    p = dict(norm_pre_mix=norm_pre_mix, w_in=w_in, lambda_q1=lambda_q1, lambda_k1=lambda_k1,
             lambda_q2=lambda_q2, lambda_k2=lambda_k2, subln_w=subln_w, conv_w=conv_w,
             conv_b=conv_b, dt_bias_f=dt_bias_f, dt_bias_b=dt_bias_b, a_log_f=a_log_f,
             a_log_b=a_log_b, d_skip=d_skip, ssm_norm_w=ssm_norm_w, w_proj_attn=w_proj_attn,
             w_proj_ssm=w_proj_ssm, w_out=w_out, norm_post_mix=norm_post_mix,
             norm_pre_ffn=norm_pre_ffn, w_gate_up=w_gate_up, w_down=w_down,
             norm_post_ffn=norm_post_ffn)
    layers = [_prep_layer(l, p) for l in range(DEPTH)]
    cache = {}

    def bias_for(tq, tk):
        if (tq, tk) not in cache:
            cache[tq, tk] = _bias_tiles(rel_bias, tq, tk)
        return cache[tq, tk]

    return (_trunk(x_prompt, layers, bias_for), _trunk(x_sample, layers, bias_for))
```

```python
import functools
import math

import jax
import jax.numpy as jnp
import numpy as np
from jax import lax
from jax.experimental import pallas as pl
from jax.experimental.pallas import tpu as pltpu

F32 = jnp.float32
BF16 = jnp.bfloat16

D_MODEL = 1024
DEPTH = 2
ATT_HEADS = 8
ATT_HEAD_DIM = 64
ATT_V_DIM = 2 * ATT_HEAD_DIM
NUM_BUCKETS = 32
MAX_DISTANCE = 128
SSM_INNER = 1024
SSM_HEAD_DIM = 64
SSM_HEADS = 16
SSM_GROUPS = 2
SSM_STATE = 128
D_CONV = 5
CONV_CH = SSM_INNER + 2 * SSM_GROUPS * SSM_STATE
D_FF = 2816
EPS = 1e-6

LANES = 128
SUBLANES = 8
DT_PAD = LANES
VMEM_LIMIT = 56 * 1024 * 1024

SSD_CHUNK = 128


def _cparams(sem):
    return pltpu.CompilerParams(dimension_semantics=sem, vmem_limit_bytes=VMEM_LIMIT)


def _rms(x, w):
    ms = jnp.mean(x * x, axis=-1, keepdims=True)
    return x * lax.rsqrt(ms + EPS) * w


def _sigmoid(x):
    return 1.0 / (1.0 + jnp.exp(-x))


def _resident(shape):
    zeros = (0,) * len(shape)
    return pl.BlockSpec(shape, lambda *_: zeros, pipeline_mode=pl.Buffered(1))


IN_SEGS = (D_MODEL, D_MODEL, D_MODEL, SSM_INNER, CONV_CH, DT_PAD, 2 * D_MODEL)
IN_DTYPES = (BF16, BF16, BF16, BF16, BF16, F32, BF16)
XBC_SEG = 4
IN_COLS_PAD = sum(IN_SEGS)
MM_COLS = 512
HALO = SUBLANES


def _inproj_kernel(x_ref, xp_ref, xn_ref, nw_ref, w_ref, cw_ref, cb_ref, *refs, tm, nt):
    out_refs, ext_ref = refs[:-1], refs[-1]
    i = pl.program_id(0)
    nw = nw_ref[...]
    h = _rms(x_ref[...], nw).astype(BF16)
    hp = _rms(xp_ref[...], nw).astype(BF16)
    hn = _rms(xn_ref[...], nw).astype(BF16)
    first = i % nt == 0
    last = i % nt == nt - 1
    offs = [sum(IN_SEGS[:seg]) for seg in range(len(IN_SEGS))]
    order = [XBC_SEG] + [seg for seg in range(len(IN_SEGS)) if seg != XBC_SEG]
    for seg in order:
        ref, n, off = out_refs[seg], IN_SEGS[seg], offs[seg]
        for c0 in range(0, n, MM_COLS):
            cw = min(MM_COLS, n - c0)
            w = w_ref[:, off + c0:off + c0 + cw]
            r = jnp.dot(h, w, preferred_element_type=F32)
            if seg != XBC_SEG:
                ref[:, c0:c0 + cw] = r.astype(ref.dtype)
                continue
            rp = jnp.dot(hp, w, preferred_element_type=F32)
            rn = jnp.dot(hn, w, preferred_element_type=F32)
            ext_ref[0:HALO, c0:c0 + cw] = jnp.where(first, 0.0, rp)
            ext_ref[HALO:HALO + tm, c0:c0 + cw] = r
            ext_ref[HALO + tm:, c0:c0 + cw] = jnp.where(last, 0.0, rn)
            acc = jnp.zeros((tm, cw), F32) + cb_ref[:, c0:c0 + cw]
            for j in range(D_CONV):
                acc = acc + (ext_ref[pl.ds(HALO - D_CONV // 2 + j, tm), c0:c0 + cw]
                             * cw_ref[j:j + 1, c0:c0 + cw])
            ref[:, c0:c0 + cw] = (acc * _sigmoid(acc)).astype(ref.dtype)


def _inproj(x, nw, w, conv_w, conv_b, *, s, tm):
    t = x.shape[0]
    r = tm // HALO
    last_blk = t // HALO - 1
    out_shape = tuple(jax.ShapeDtypeStruct((t, n), d) for n, d in zip(IN_SEGS, IN_DTYPES))
    out_specs = tuple(pl.BlockSpec((tm, n), lambda i: (i, 0)) for n in IN_SEGS)
    return pl.pallas_call(
        functools.partial(_inproj_kernel, tm=tm, nt=s // tm),
        grid=(t // tm,),
        in_specs=[pl.BlockSpec((tm, D_MODEL), lambda i: (i, 0)),
                  pl.BlockSpec((HALO, D_MODEL), lambda i: (jnp.maximum(i * r - 1, 0), 0)),
                  pl.BlockSpec((HALO, D_MODEL), lambda i: (jnp.minimum((i + 1) * r, last_blk), 0)),
                  _resident((1, D_MODEL)),
                  _resident((D_MODEL, IN_COLS_PAD)),
                  _resident((D_CONV, CONV_CH)),
                  _resident((1, CONV_CH))],
        out_specs=out_specs,
        out_shape=out_shape,
        scratch_shapes=[pltpu.VMEM((tm + 2 * HALO, CONV_CH), F32)],
        compiler_params=_cparams(("parallel",)),
        name="inproj",
    )(x, x, x, nw, w, conv_w, conv_b)


LOG2E = math.log2(math.e)


def _bias_kernel(tab_ref, ids_ref, o_ref):
    h = pl.program_id(0)
    ids = ids_ref[0]
    acc = jnp.zeros(ids.shape, F32)
    for n in range(NUM_BUCKETS):
        acc = jnp.where(ids == n, tab_ref[n, h], acc)
    o_ref[0, 0] = acc * LOG2E


def _bias_offsets(tq, tk):
    assert tk % tq == 0 and tq >= MAX_DISTANCE
    return -(tk // tq + 1), 2


def _bias_tiles(rel_bias, tq, tk):
    ids = _bucket_ids(tq, tk)
    nbias = ids.shape[0]
    return pl.pallas_call(
        _bias_kernel,
        grid=(ATT_HEADS, nbias),
        in_specs=[pl.BlockSpec(memory_space=pltpu.SMEM),
                  pl.BlockSpec((1, tq, tk), lambda h, d: (d, 0, 0))],
        out_specs=pl.BlockSpec((1, 1, tq, tk), lambda h, d: (h, d, 0, 0)),
        out_shape=jax.ShapeDtypeStruct((ATT_HEADS, nbias, tq, tk), F32),
        compiler_params=_cparams(("parallel", "parallel")),
        name="bias_tiles",
    )(rel_bias, ids)


def _bucket_ids(tq, tk):
    lo, hi = _bias_offsets(tq, tk)
    d = jnp.arange(lo, hi + 1, dtype=jnp.int32)[:, None, None] * tq
    rel = d + jnp.arange(tk, dtype=jnp.int32)[None, None, :] - jnp.arange(tq, dtype=jnp.int32)[None, :, None]
    nb = NUM_BUCKETS // 2
    ret = (rel > 0).astype(jnp.int32) * nb
    n = jnp.abs(rel)
    max_exact = nb // 2
    nf = jnp.maximum(n, 1).astype(F32)
    large = max_exact + (jnp.log(nf / max_exact) / math.log(MAX_DISTANCE / max_exact)
                         * (nb - max_exact)).astype(jnp.int32)
    large = jnp.minimum(large, nb - 1)
    return ret + jnp.where(n < max_exact, n, large)


def _attn_kernel(q_ref, k_ref, v_ref, b_ref, lq1_ref, lk1_ref, lq2_ref, lk2_ref, sw_ref, o_ref,
                 s_ref, m_ref, acc_ref, vext_ref, *, tq, tk, nk, lam_init):
    qi = pl.program_id(2)
    lo, hi = _bias_offsets(tq, tk)
    nfold = tk // LANES

    @pl.when(qi == 0)
    def _():
        vext_ref[:, 0:ATT_V_DIM] = v_ref[...]
        vext_ref[:, ATT_V_DIM:] = jnp.ones((vext_ref.shape[0], LANES), BF16)

    q = q_ref[...]
    lane = lax.broadcasted_iota(jnp.int32, q.shape, 1)
    zero = jnp.zeros_like(q)
    qs = (jnp.where(lane < ATT_HEAD_DIM, q, zero), jnp.where(lane >= ATT_HEAD_DIM, q, zero))

    for mp in range(2):
        fold = None
        for kj in range(nk):
            k = k_ref[kj * tk:(kj + 1) * tk, :]
            bias = b_ref[0, jnp.clip((tk // tq) * kj - qi, lo, hi) - lo]
            s = lax.dot_general(qs[mp], k, (((1,), (1,)), ((), ())), preferred_element_type=F32) + bias
            s_ref[mp, kj] = s
            for c in range(nfold):
                part = s[:, c * LANES:(c + 1) * LANES]
                fold = part if fold is None else jnp.maximum(fold, part)
        m_ref[mp] = jnp.broadcast_to(jnp.max(fold, axis=1, keepdims=True), (tq, LANES))

    for mp in range(2):
        acc = None
        for kj in range(nk):
            m = m_ref[mp]
            p = jnp.exp2(s_ref[mp, kj] - jnp.concatenate([m] * nfold, axis=1))
            pv = jnp.dot(p.astype(BF16), vext_ref[kj * tk:(kj + 1) * tk, :], preferred_element_type=F32)
            acc = pv if acc is None else acc + pv
        acc_ref[mp] = acc

    lam = (jnp.exp(jnp.sum(lq1_ref[...] * lk1_ref[...], axis=1, keepdims=True))
           - jnp.exp(jnp.sum(lq2_ref[...] * lk2_ref[...], axis=1, keepdims=True)) + lam_init)
    o = (acc_ref[0, :, 0:ATT_V_DIM] / acc_ref[0, :, ATT_V_DIM:]
         - lam * (acc_ref[1, :, 0:ATT_V_DIM] / acc_ref[1, :, ATT_V_DIM:]))
    o_ref[...] = (_rms(o, sw_ref[...]) * (1.0 - lam_init)).astype(o_ref.dtype)


def _attention(q, k, v, bias, lq1, lk1, lq2, lk2, subln_w, *, b, s, tq, tk, lam_init):
    t = b * s
    nq = s // tq
    nk = s // tk
    nbias = bias.shape[1]
    kern = functools.partial(_attn_kernel, tq=tq, tk=tk, nk=nk, lam_init=lam_init)
    vec = pl.BlockSpec((1, ATT_HEAD_DIM), lambda bi, h, qi: (0, 0))
    return pl.pallas_call(
        kern,
        grid=(b, ATT_HEADS, nq),
        in_specs=[pl.BlockSpec((tq, LANES), lambda bi, h, qi: (bi * nq + qi, h)),
                  pl.BlockSpec((s, LANES), lambda bi, h, qi: (bi, h)),
                  pl.BlockSpec((s, LANES), lambda bi, h, qi: (bi, h)),
                  pl.BlockSpec((1, nbias, tq, tk), lambda bi, h, qi: (h, 0, 0, 0)),
                  vec, vec, vec, vec,
                  pl.BlockSpec((1, ATT_V_DIM), lambda bi, h, qi: (0, 0))],
        out_specs=pl.BlockSpec((tq, LANES), lambda bi, h, qi: (bi * nq + qi, h)),
        out_shape=jax.ShapeDtypeStruct((t, ATT_HEADS * ATT_V_DIM), BF16),
        scratch_shapes=[pltpu.VMEM((2, nk, tq, tk), F32),
                        pltpu.VMEM((2, tq, LANES), F32),
                        pltpu.VMEM((2, tq, ATT_V_DIM + LANES), F32),
                        pltpu.VMEM((s, ATT_V_DIM + LANES), BF16)],
        compiler_params=_cparams(("parallel", "parallel", "arbitrary")),
        name="diff_attn",
    )(q, k, v, bias, lq1, lk1, lq2, lk2, subln_w)


def _split(x, parts):
    out = []
    for _ in range(parts - 1):
        hi = x.astype(BF16)
        out.append(hi)
        x = x - hi.astype(F32)
    out.append(x.astype(BF16))
    return out


def _dot01_right(x, m01, parts=3):
    return sum(jnp.dot(p, m01, preferred_element_type=F32) for p in _split(x, parts))


def _dot01_left(m01, x, parts=3):
    return sum(jnp.dot(m01, p, preferred_element_type=F32) for p in _split(x, parts))


def _ssd_chunk(d, xc_ref, dt_ref, dtb_ref, alog_ref, tri_ref, e1, y_ref, st_ref):
    L = SSD_CHUNK
    H = SSM_HEADS
    GW = SSM_INNER // SSM_GROUPS

    xin = dt_ref[:, d * H:(d + 1) * H] + dtb_ref[d:d + 1, :]
    dt = jnp.maximum(xin, 0.0) + jnp.log(1.0 + jnp.exp(-jnp.abs(xin)))
    a = dt * (-jnp.exp(alog_ref[d:d + 1, :]))

    tri_d = tri_ref[d]
    tri_t = tri_ref[1 - d]
    mask = tri_d > 0
    u = _dot01_left(tri_d, a)
    u_t = _dot01_right(a.T, tri_t)
    tot = jnp.sum(a, axis=0, keepdims=True)

    dts_full = _dot01_right(dt * jnp.exp(tot - u), e1, parts=2)
    dec_full = _dot01_right(jnp.broadcast_to(jnp.exp(tot), (SUBLANES, H)), e1)[0:1, :]
    v_t = u_t - jnp.log(dt.T)

    lane = lax.broadcasted_iota(jnp.int32, (L, LANES), 1)
    left = lane < SSM_HEAD_DIM
    hpg = H // SSM_GROUPS

    zero = jnp.zeros((L, LANES), BF16)

    for g in range(SSM_GROUPS):
        bg = xc_ref[:, SSM_INNER + g * SSM_STATE:SSM_INNER + (g + 1) * SSM_STATE].astype(BF16)
        c0 = SSM_INNER + SSM_GROUPS * SSM_STATE + g * SSM_STATE
        cg = xc_ref[:, c0:c0 + SSM_STATE].astype(BF16)
        cb = lax.dot_general(cg, bg, (((1,), (1,)), ((), ())), preferred_element_type=F32)
        yoff = jnp.dot(cg, st_ref[d, :, g * GW:(g + 1) * GW].astype(BF16), preferred_element_type=F32)
        xdts = []
        for j in range(hpg // 2):
            col = g * GW + j * LANES
            xp = xc_ref[:, col:col + LANES]
            xpb = xp.astype(BF16)
            rhs = jnp.concatenate([jnp.where(left, xpb, zero), jnp.where(left, zero, xpb)], axis=0)
            ucols, mhs = [], []
            for side in range(2):
                h = g * hpg + 2 * j + side
                ucol = jnp.broadcast_to(u[:, h:h + 1], (L, L))
                ucols.append(ucol)
                mh = cb * jnp.exp(jnp.where(mask, ucol - v_t[h:h + 1, :], -jnp.inf))
                mhs.append(mh.astype(BF16))
            ydiag = jnp.dot(jnp.concatenate(mhs, axis=1), rhs, preferred_element_type=F32)
            osc = jnp.exp(jnp.where(left, ucols[0], ucols[1]))
            y_ref[:, col:col + LANES] = (ydiag + osc * yoff[:, j * LANES:(j + 1) * LANES]).astype(y_ref.dtype)
            xdts.append((xp * dts_full[:, col:col + LANES]).astype(BF16))
        upd = lax.dot_general(bg, jnp.concatenate(xdts, axis=1), (((0,), (0,)), ((), ())),
                              preferred_element_type=F32)
        st_ref[d, :, g * GW:(g + 1) * GW] = (st_ref[d, :, g * GW:(g + 1) * GW]
                                             * dec_full[:, g * GW:(g + 1) * GW] + upd)


def _ssd_kernel(xf_ref, xb_ref, dtf_ref, dtr_ref, dtb_ref, alog_ref, tri_ref, e1_ref, yf_ref, yb_ref,
                st_ref):
    @pl.when(pl.program_id(1) == 0)
    def _():
        st_ref[...] = jnp.zeros(st_ref.shape, F32)

    e1 = e1_ref[...]
    _ssd_chunk(0, xf_ref, dtf_ref, dtb_ref, alog_ref, tri_ref, e1, yf_ref, st_ref)
    _ssd_chunk(1, xb_ref, dtr_ref, dtb_ref, alog_ref, tri_ref, e1, yb_ref, st_ref)


def _ssd(xconv, dt_raw, dt_bias, a_log, *, b, s):
    t = b * s
    L = SSD_CHUNK
    nc = s // L
    tri = np.tril(np.ones((L, L), np.float32))
    tri2 = jnp.asarray(np.stack([tri, tri.T]), BF16)
    heads = np.arange(SSM_HEADS)[:, None]
    e1 = jnp.asarray(np.arange(SSM_INNER)[None, :] // SSM_HEAD_DIM == heads, BF16)
    fwd = lambda bi, c: (bi * nc + c, 0)
    bwd = lambda bi, c: (bi * nc + nc - 1 - c, 0)
    const2 = lambda bi, c: (0, 0)
    return pl.pallas_call(
        _ssd_kernel,
        grid=(b, nc),
        in_specs=[pl.BlockSpec((L, CONV_CH), fwd), pl.BlockSpec((L, CONV_CH), bwd),
                  pl.BlockSpec((L, DT_PAD), fwd), pl.BlockSpec((L, DT_PAD), bwd),
                  pl.BlockSpec((2, SSM_HEADS), const2),
                  pl.BlockSpec((2, SSM_HEADS), const2),
                  pl.BlockSpec((2, L, L), lambda bi, c: (0, 0, 0)),
                  pl.BlockSpec((SSM_HEADS, SSM_INNER), const2)],
        out_specs=(pl.BlockSpec((L, SSM_INNER), fwd), pl.BlockSpec((L, SSM_INNER), bwd)),
        out_shape=(jax.ShapeDtypeStruct((t, SSM_INNER), BF16), jax.ShapeDtypeStruct((t, SSM_INNER), BF16)),
        scratch_shapes=[pltpu.VMEM((2, SSM_STATE, SSM_INNER), F32)],
        compiler_params=_cparams(("parallel", "arbitrary")),
        name="ssd_scan",
    )(xconv, xconv, dt_raw, dt_raw, dt_bias, a_log, tri2, e1)


def _merge_kernel(x_ref, yf_ref, yb_ref, xs_ref, z_ref, att_ref, g_ref, dsk_ref, snw_ref,
                  wpa_ref, wps_ref, wo_ref, pnw_ref, o_ref):
    z = z_ref[...].astype(F32)
    y = yf_ref[...].astype(F32) + yb_ref[...].astype(F32) + dsk_ref[...] * xs_ref[...].astype(F32)
    y = y * (z * _sigmoid(z))
    y_ssm = _rms(y, snw_ref[...]).astype(BF16)
    pa = jnp.dot(att_ref[...], wpa_ref[...], preferred_element_type=F32)
    ps = jnp.dot(y_ssm, wps_ref[...], preferred_element_type=F32)
    merged = (_sigmoid(g_ref[:, 0:D_MODEL].astype(F32)) * pa
              + _sigmoid(g_ref[:, D_MODEL:2 * D_MODEL].astype(F32)) * ps)
    m = jnp.dot(merged.astype(BF16), wo_ref[...], preferred_element_type=F32)
    o_ref[...] = x_ref[...] + _rms(m, pnw_ref[...])


def _merge(x, yf, yb, xconv, z, att, gates, dsk, snw, wpa, wps, wo, pnw, tm):
    t = x.shape[0]
    row = lambda n: pl.BlockSpec((tm, n), lambda i: (i, 0))
    return pl.pallas_call(
        _merge_kernel,
        grid=(t // tm,),
        in_specs=[row(D_MODEL), row(SSM_INNER), row(SSM_INNER),
                  row(SSM_INNER),
                  row(SSM_INNER), row(ATT_HEADS * ATT_V_DIM), row(2 * D_MODEL),
                  _resident((1, SSM_INNER)), _resident((1, SSM_INNER)),
                  _resident((D_MODEL, D_MODEL)), _resident((D_MODEL, D_MODEL)),
                  _resident((D_MODEL, D_MODEL)), _resident((1, D_MODEL))],
        out_specs=row(D_MODEL),
        out_shape=jax.ShapeDtypeStruct((t, D_MODEL), F32),
        compiler_params=_cparams(("parallel",)),
        name="merge_proj",
    )(x, yf, yb, xconv, z, att, gates, dsk, snw, wpa, wps, wo, pnw)


FF_COLS = 256


def _ffn_kernel(x_ref, n1_ref, wgu_ref, wd_ref, n2_ref, o_ref, acc_ref):
    x = x_ref[...]
    h = _rms(x, n1_ref[...]).astype(BF16)
    for j in range(D_FF // FF_COLS):
        c0 = j * FF_COLS
        gate = jnp.dot(h, wgu_ref[:, c0:c0 + FF_COLS], preferred_element_type=F32)
        up = jnp.dot(h, wgu_ref[:, D_FF + c0:D_FF + c0 + FF_COLS], preferred_element_type=F32)
        act = (gate * _sigmoid(gate) * up).astype(BF16)
        part = jnp.dot(act, wd_ref[c0:c0 + FF_COLS, :], preferred_element_type=F32)
        if j == 0:
            acc_ref[...] = part
        else:
            acc_ref[...] += part
    o_ref[...] = x + _rms(acc_ref[...], n2_ref[...])


def _ffn(x, n1, wgu, wd, n2, tm):
    t = x.shape[0]
    return pl.pallas_call(
        _ffn_kernel,
        grid=(t // tm,),
        in_specs=[pl.BlockSpec((tm, D_MODEL), lambda i: (i, 0)),
                  _resident((1, D_MODEL)),
                  _resident((D_MODEL, 2 * D_FF)),
                  _resident((D_FF, D_MODEL)),
                  _resident((1, D_MODEL))],
        out_specs=pl.BlockSpec((tm, D_MODEL), lambda i: (i, 0)),
        out_shape=jax.ShapeDtypeStruct((t, D_MODEL), F32),
        scratch_shapes=[pltpu.VMEM((tm, D_MODEL), F32)],
        compiler_params=_cparams(("parallel",)),
        name="swiglu_ffn",
    )(x, n1, wgu, wd, n2)


def _tiles(b, s):
    t = b * s
    return dict(tm=min(512, s), tq=min(512, s), tk=min(512, s))


def _prep_layer(l, p):
    w_in = p['w_in'][l]
    q_cols = ATT_HEADS * 2 * ATT_HEAD_DIM
    dt0 = 3 * q_cols + SSM_INNER + CONV_CH
    scale = ATT_HEAD_DIM ** -0.5 * LOG2E
    w_pad = jnp.concatenate(
        [w_in[:, :q_cols] * scale, w_in[:, q_cols:dt0], w_in[:, dt0:dt0 + 2 * SSM_HEADS],
         jnp.zeros((D_MODEL, DT_PAD - 2 * SSM_HEADS), F32), w_in[:, dt0 + 2 * SSM_HEADS:]], axis=1)
    row = lambda v: v.reshape(1, -1)
    return dict(
        norm_pre_mix=row(p['norm_pre_mix'][l]), w_in=w_pad.astype(BF16),
        lq1=row(p['lambda_q1'][l]), lk1=row(p['lambda_k1'][l]),
        lq2=row(p['lambda_q2'][l]), lk2=row(p['lambda_k2'][l]),
        subln_w=row(p['subln_w'][l]),
        conv_w=p['conv_w'][l].reshape(D_CONV, CONV_CH), conv_b=row(p['conv_b'][l]),
        dt_bias=jnp.stack([p['dt_bias_f'][l], p['dt_bias_b'][l]]),
        a_log=jnp.stack([p['a_log_f'][l], p['a_log_b'][l]]),
        d_skip=row(jnp.repeat(p['d_skip'][l], SSM_HEAD_DIM)),
        ssm_norm_w=row(p['ssm_norm_w'][l]),
        w_proj_attn=p['w_proj_attn'][l].astype(BF16), w_proj_ssm=p['w_proj_ssm'][l].astype(BF16),
        w_out=p['w_out'][l].astype(BF16), norm_post_mix=row(p['norm_post_mix'][l]),
        norm_pre_ffn=row(p['norm_pre_ffn'][l]), w_gate_up=p['w_gate_up'][l].astype(BF16),
        w_down=p['w_down'][l].astype(BF16), norm_post_ffn=row(p['norm_post_ffn'][l]))


def _trunk(x3, layers, bias_for):
    b, s, _ = x3.shape
    tl = _tiles(b, s)
    x = x3.reshape(b * s, D_MODEL)
    bias = bias_for(tl['tq'], tl['tk'])
    for l, w in enumerate(layers):
        lam_init = 0.8 - 0.6 * math.exp(-0.3 * l)
        q, k, v, z, xconv, dt_raw, gates = _inproj(x, w['norm_pre_mix'], w['w_in'], w['conv_w'],
                                                   w['conv_b'], s=s, tm=tl['tm'])
        att = _attention(q, k, v, bias, w['lq1'], w['lk1'], w['lq2'], w['lk2'], w['subln_w'],
                         b=b, s=s, tq=tl['tq'], tk=tl['tk'], lam_init=lam_init)
        yf, yb = _ssd(xconv, dt_raw, w['dt_bias'], w['a_log'], b=b, s=s)
        x = _merge(x, yf, yb, xconv, z, att, gates, w['d_skip'], w['ssm_norm_w'], w['w_proj_attn'],
                   w['w_proj_ssm'], w['w_out'], w['norm_post_mix'], tl['tm'])
        x = _ffn(x, w['norm_pre_ffn'], w['w_gate_up'], w['w_down'], w['norm_post_ffn'], tl['tm'])
    return x.reshape(b, s, D_MODEL)


def kernel(x_prompt, x_sample, rel_bias, norm_pre_mix, w_in, lambda_q1, lambda_k1, lambda_q2,
           lambda_k2, subln_w, conv_w, conv_b, dt_bias_f, dt_bias_b, a_log_f, a_log_b, d_skip,
           ssm_norm_w, w_proj_attn, w_proj_ssm, w_out, norm_post_mix, norm_pre_ffn, w_gate_up,
           w_down, norm_post_ffn):
    p = dict(norm_pre_mix=norm_pre_mix, w_in=w_in, lambda_q1=lambda_q1, lambda_k1=lambda_k1,
             lambda_q2=lambda_q2, lambda_k2=lambda_k2, subln_w=subln_w, conv_w=conv_w,
             conv_b=conv_b, dt_bias_f=dt_bias_f, dt_bias_b=dt_bias_b, a_log_f=a_log_f,
             a_log_b=a_log_b, d_skip=d_skip, ssm_norm_w=ssm_norm_w, w_proj_attn=w_proj_attn,
             w_proj_ssm=w_proj_ssm, w_out=w_out, norm_post_mix=norm_post_mix,
             norm_pre_ffn=norm_pre_ffn, w_gate_up=w_gate_up, w_down=w_down,
             norm_post_ffn=norm_post_ffn)
    layers = [_prep_layer(l, p) for l in range(DEPTH)]
    cache = {}

    def bias_for(tq, tk):
        if (tq, tk) not in cache:
            cache[tq, tk] = _bias_tiles(rel_bias, tq, tk)
        return cache[tq, tk]

    return (_trunk(x_prompt, layers, bias_for), _trunk(x_sample, layers, bias_for))
```
